```python
import math
import numpy as np
import jax
import jax.numpy as jnp
from jax import lax

D_MODEL = 1024
BATCH = 4
SEQ = 4096
DEPTH = 4

HEAD_DIM = 64
Q_BLOCK = 128
EPS = 1e-6
NEG = -1e30

DIFF_HEADS = D_MODEL // 256
DIFF_VDIM = 2 * HEAD_DIM

NSA_HEADS = D_MODEL // 128
NSA_GROUPS = max(1, NSA_HEADS // 4)
NSA_HPG = NSA_HEADS // NSA_GROUPS
CMP_BLOCK = 32
CMP_STRIDE = 16
CMP_HIDDEN = 2 * HEAD_DIM
SLC_BLOCK = 64
SLC_TOPK = 16
WINDOW = 512
FORCE_BONUS = 1e4

FOX_HEADS = D_MODEL // HEAD_DIM

N_EXPERTS = 32
TOP_K = 4
D_FF = D_MODEL
SWIGLU_LIMIT = 7.0
SWIGLU_ALPHA = 1.702
MOE_BLOCK = 512

N_EVEN = (DEPTH + 1) // 2
N_ODD = DEPTH // 2

DIFF_QK = DIFF_HEADS * 2 * HEAD_DIM
DIFF_V = DIFF_HEADS * DIFF_VDIM
NSA_Q = NSA_HEADS * HEAD_DIM
NSA_KV = NSA_GROUPS * HEAD_DIM
NSA_GATE = 3 * NSA_HEADS
EVEN_WIDTHS = (DIFF_QK, DIFF_QK, DIFF_V, NSA_Q, NSA_KV, NSA_KV, NSA_KV, NSA_KV, NSA_KV, NSA_KV, NSA_GATE)
EVEN_IN = sum(EVEN_WIDTHS)
EVEN_MIX = DIFF_V + NSA_Q
FOX_W = FOX_HEADS * HEAD_DIM
ODD_IN = 3 * FOX_W + FOX_HEADS

kernel_name = 'hybrid_diff_nsa_fox_moe_adaln'


def rmsnorm(x, g):
    xf = x.astype(jnp.float32)
    y = xf * lax.rsqrt(jnp.mean(xf * xf, axis=-1, keepdims=True) + EPS)
    return (y * g.astype(jnp.float32)).astype(x.dtype)


def alibi_slopes(n):
    return jnp.asarray(2.0 ** (-8.0 * np.arange(1, n + 1) / n), dtype=jnp.float32)


def masked_softmax(s, mask):
    s = jnp.where(mask, s.astype(jnp.float32), NEG)
    m = jnp.max(s, axis=-1, keepdims=True)
    p = jnp.where(mask, jnp.exp(s - m), 0.0)
    return p / jnp.maximum(jnp.sum(p, axis=-1, keepdims=True), 1e-30)


def sweep_query_blocks(fn, *arrays):
    B, S = arrays[0].shape[:2]
    nb = S // Q_BLOCK
    blocked = tuple(jnp.swapaxes(a.reshape(B, nb, Q_BLOCK, *a.shape[2:]), 0, 1) for a in arrays)
    out = lax.map(lambda args: fn(args[0], *args[1:]), (jnp.arange(nb), *blocked))
    return jnp.swapaxes(out, 0, 1).reshape(B, S, *out.shape[3:])


def diff_attention(q, k, v, lam, lam_init, subln_g):
    B, S, H, _, Dh = q.shape
    scale = Dh ** -0.5
    slopes = alibi_slopes(H)
    kpos = jnp.arange(S)

    def block(i, qb):
        qpos = i * Q_BLOCK + jnp.arange(Q_BLOCK)
        dist = (qpos[:, None] - kpos[None, :]).astype(jnp.float32)
        bias = -slopes[:, None, None] * jnp.abs(dist)
        s = jnp.einsum('bqhcd,bkhcd->bhcqk', qb, k).astype(jnp.float32) * scale + bias[None, :, None]
        p = jax.nn.softmax(jnp.where(dist >= 0, s, NEG), axis=-1)
        p = p[:, :, 0] - lam * p[:, :, 1]
        return jnp.einsum('bhqk,bkhe->bqhe', p.astype(v.dtype), v)

    o = sweep_query_blocks(block, q)
    o = rmsnorm(o, subln_g) * (1.0 - lam_init)
    return o.reshape(B, S, H * v.shape[-1])


def compress_blocks(x, blk_idx, pos_emb, w1, w2):
    B, _, G, Dh = x.shape
    blocks = x[:, blk_idx] + pos_emb[:, None, :]
    blocks = jnp.moveaxis(blocks, 3, 2).reshape(B, blk_idx.shape[0], G, -1)
    return jax.nn.silu(blocks @ w1) @ w2


def nsa_attention(q, kc, vc, ks, vs, kw, vw, gate_logits, cmp_pos, cmp_w1, cmp_w2):
    B, S, G, Hg, Dh = q.shape
    scale = Dh ** -0.5
    slopes = alibi_slopes(G * Hg).reshape(G, Hg)
    pos = jnp.arange(S)

    n_cmp = (S - CMP_BLOCK) // CMP_STRIDE + 1
    blk_idx = np.arange(n_cmp)[:, None] * CMP_STRIDE + np.arange(CMP_BLOCK)[None, :]
    k_cmp = compress_blocks(kc, blk_idx, cmp_pos[0], cmp_w1[0], cmp_w2[0])
    v_cmp = compress_blocks(vc, blk_idx, cmp_pos[1], cmp_w1[1], cmp_w2[1])
    cend = jnp.asarray(blk_idx[:, -1])
    dist_c = (pos[:, None] - cend[None, :]).astype(jnp.float32)
    s_c = jnp.einsum('bsghd,bngd->bghsn', q, k_cmp).astype(jnp.float32) * scale \
        - slopes[:, :, None, None] * dist_c
    p_c = masked_softmax(s_c, dist_c >= 0)
    o_cmp = jnp.einsum('bghsn,bngd->bsghd', p_c.astype(v_cmp.dtype), v_cmp)

    n_slc = S // SLC_BLOCK
    top = min(SLC_TOPK, n_slc)
    ci = np.arange(n_cmp)[:, None] * CMP_STRIDE
    sj = np.arange(n_slc)[None, :] * SLC_BLOCK
    overlap = jnp.asarray(((ci < sj + SLC_BLOCK) & (ci + CMP_BLOCK > sj)).astype(np.float32))
    imp = jnp.einsum('bghsn,nj->bgsj', p_c, overlap)
    blk = jnp.arange(n_slc)[None, :]
    cur = (pos // SLC_BLOCK)[:, None]
    avail = blk <= cur
    forced = (blk == 0) | (blk == cur) | (blk == cur - 1)
    score = jnp.where(avail, imp + jnp.where(forced, FORCE_BONUS, 0.0), NEG)
    top_s, top_i = lax.top_k(score, top)
    top_i = jnp.swapaxes(top_i, 1, 2)
    top_valid = jnp.swapaxes(top_s > 0.5 * NEG, 1, 2)

    ks_b = jnp.moveaxis(ks.reshape(B, n_slc, SLC_BLOCK, G, Dh), 3, 1)
    vs_b = jnp.moveaxis(vs.reshape(B, n_slc, SLC_BLOCK, G, Dh), 3, 1)
    bi = jnp.arange(B)[:, None, None, None]
    gi = jnp.arange(G)[None, :, None, None]
    within = jnp.arange(SLC_BLOCK)

    def slc_block(i, qb, ib, vb):
        ib = jnp.swapaxes(ib, 1, 2)
        vb = jnp.swapaxes(vb, 1, 2)
        kg = ks_b[bi, gi, ib]
        vg = vs_b[bi, gi, ib]
        qpos = i * Q_BLOCK + jnp.arange(Q_BLOCK)
        kpos = ib[..., None] * SLC_BLOCK + within
        d = (qpos[None, None, :, None, None] - kpos).astype(jnp.float32)
        mask = (vb[..., None] & (d >= 0)).reshape(B, G, 1, Q_BLOCK, top * SLC_BLOCK)
        s = jnp.einsum('btghd,bgtmld->bghtml', qb, kg).astype(jnp.float32) * scale \
            - slopes[None, :, :, None, None, None] * d[:, :, None]
        p = masked_softmax(s.reshape(B, G, Hg, Q_BLOCK, top * SLC_BLOCK), mask)
        vg = vg.reshape(B, G, Q_BLOCK, top * SLC_BLOCK, Dh)
        return jnp.einsum('bghtk,bgtkd->btghd', p.astype(vg.dtype), vg)

    o_slc = sweep_query_blocks(slc_block, q, top_i, top_valid)

    kw_p = jnp.pad(kw, ((0, 0), (WINDOW, 0), (0, 0), (0, 0)))
    vw_p = jnp.pad(vw, ((0, 0), (WINDOW, 0), (0, 0), (0, 0)))
    span = WINDOW + Q_BLOCK

    def win_block(i, qb):
        start = i * Q_BLOCK
        kb = lax.dynamic_slice_in_dim(kw_p, start, span, axis=1)
        vb = lax.dynamic_slice_in_dim(vw_p, start, span, axis=1)
        qpos = start + jnp.arange(Q_BLOCK)
        kpos = start - WINDOW + jnp.arange(span)
        d = (qpos[:, None] - kpos[None, :]).astype(jnp.float32)
        mask = (d >= 0) & (d < WINDOW) & (kpos >= 0)[None, :]
        s = jnp.einsum('btghd,bkgd->bghtk', qb, kb).astype(jnp.float32) * scale \
            - slopes[:, :, None, None] * d
        p = masked_softmax(s, mask)
        return jnp.einsum('bghtk,bkgd->btghd', p.astype(vb.dtype), vb)

    o_win = sweep_query_blocks(win_block, q)

    g = jax.nn.sigmoid(gate_logits.astype(jnp.float32))
    o = g[..., 0:1] * o_cmp + g[..., 1:2] * o_slc + g[..., 2:3] * o_win
    return o.astype(q.dtype).reshape(B, S, G * Hg * Dh)


def even_mixer(h, w_in, w_out, lam_p, subln_g, cmp_pos, cmp_w1, cmp_w2, gate_b, lam_init):
    B, S, _ = h.shape
    splits = np.cumsum(EVEN_WIDTHS)[:-1].tolist()
    aq, ak, av, bq, bkc, bvc, bks, bvs, bkw, bvw, bg = jnp.split(h @ w_in, splits, axis=-1)
    aq = aq.reshape(B, S, DIFF_HEADS, 2, HEAD_DIM)
    ak = ak.reshape(B, S, DIFF_HEADS, 2, HEAD_DIM)
    av = av.reshape(B, S, DIFF_HEADS, DIFF_VDIM)
    lam_p = lam_p.astype(jnp.float32)
    lam = jnp.exp(jnp.sum(lam_p[0] * lam_p[1])) - jnp.exp(jnp.sum(lam_p[2] * lam_p[3])) + lam_init
    o_a = diff_attention(aq, ak, av, lam, lam_init, subln_g)

    def kv(t):
        return t.reshape(B, S, NSA_GROUPS, HEAD_DIM)

    o_b = nsa_attention(bq.reshape(B, S, NSA_GROUPS, NSA_HPG, HEAD_DIM),
                        kv(bkc), kv(bvc), kv(bks), kv(bvs), kv(bkw), kv(bvw),
                        (bg + gate_b).reshape(B, S, NSA_GROUPS, NSA_HPG, 3),
                        cmp_pos, cmp_w1, cmp_w2)
    return jnp.concatenate([o_a, o_b], axis=-1) @ w_out


def forgetting_attention(q, k, v, f_logit):
    B, S, H, Dh = q.shape
    scale = Dh ** -0.5
    cum = jnp.cumsum(jax.nn.log_sigmoid(f_logit.astype(jnp.float32)), axis=1)
    cum_k = jnp.swapaxes(cum, 1, 2)
    kpos = jnp.arange(S)

    def block(i, qb, cb):
        qpos = i * Q_BLOCK + jnp.arange(Q_BLOCK)
        decay = jnp.swapaxes(cb, 1, 2)[..., None] - cum_k[:, :, None, :]
        s = jnp.einsum('bqhd,bkhd->bhqk', qb, k).astype(jnp.float32) * scale + decay
        s = jnp.where((qpos[:, None] >= kpos[None, :]), s, NEG)
        p = jax.nn.softmax(s, axis=-1)
        return jnp.einsum('bhqk,bkhd->bqhd', p.astype(v.dtype), v)

    return sweep_query_blocks(block, q, cum)


def odd_mixer(h, w_in, w_out, f_b):
    B, S, _ = h.shape
    q, k, v, f = jnp.split(h @ w_in, [FOX_W, 2 * FOX_W, 3 * FOX_W], axis=-1)
    shp = (B, S, FOX_HEADS, HEAD_DIM)
    o = forgetting_attention(q.reshape(shp), k.reshape(shp), v.reshape(shp), f + f_b)
    return o.reshape(B, S, FOX_W) @ w_out


def moe(h, router_w, router_b, w_gate, b_gate, w_up, b_up, w_down, b_down):
    B, S, D = h.shape
    x = h.reshape(-1, D)
    T = x.shape[0]
    logits = (x @ router_w + router_b).astype(jnp.float32)
    top_val, top_idx = lax.top_k(logits, TOP_K)
    gate = jax.nn.softmax(top_val, axis=-1)
    e_flat = top_idx.reshape(-1)
    g_flat = gate.reshape(-1)
    tok_flat = jnp.repeat(jnp.arange(T), TOP_K)
    M = T * TOP_K
    order = jnp.argsort(e_flat)
    e_sorted = e_flat[order]
    counts = jnp.bincount(e_flat, length=N_EXPERTS)
    starts = jnp.cumsum(counts) - counts
    padded = ((counts + MOE_BLOCK - 1) // MOE_BLOCK) * MOE_BLOCK
    pends = jnp.cumsum(padded)
    pstarts = pends - padded
    dest = pstarts[e_sorted] + (jnp.arange(M) - starts[e_sorted])
    P = -(-M // MOE_BLOCK) * MOE_BLOCK + N_EXPERTS * MOE_BLOCK
    nb = P // MOE_BLOCK
    tok_pad = jnp.zeros((P,), jnp.int32).at[dest].set(tok_flat[order].astype(jnp.int32))
    gate_pad = jnp.zeros((P,), jnp.float32).at[dest].set(g_flat[order])
    block_expert = jnp.minimum(
        jnp.searchsorted(pends, jnp.arange(nb) * MOE_BLOCK, side='right'), N_EXPERTS - 1)
    xg = x[tok_pad].reshape(nb, MOE_BLOCK, D)

    def expert_block(args):
        xb, e = args
        hg = jnp.minimum(xb @ w_gate[e] + b_gate[e], SWIGLU_LIMIT)
        hu = jnp.clip(xb @ w_up[e] + b_up[e], -SWIGLU_LIMIT, SWIGLU_LIMIT)
        a = hg * jax.nn.sigmoid(SWIGLU_ALPHA * hg) * (hu + 1.0)
        return a @ w_down[e] + b_down[e]

    y = lax.map(expert_block, (xg, block_expert)).reshape(P, D)
    out = jnp.zeros_like(x).at[tok_pad].add(y * gate_pad[:, None].astype(y.dtype))
    return out.reshape(B, S, D)


def setup_inputs(seed: int = 0) -> dict:
    key = jax.random.key(seed)
    k = jax.random.split(key, 26)

    def nrm(kk, shape, scale):
        return scale * jax.random.normal(kk, shape, jnp.float32)

    D, E, F = D_MODEL, N_EXPERTS, D_FF
    return {
        'x': nrm(k[0], (BATCH, SEQ, D), 1.0),
        'c': nrm(k[1], (BATCH, D), 1.0),
        'ada_w': nrm(k[2], (DEPTH, D, 6 * D), 0.5 * D ** -0.5),
        'ada_b': nrm(k[3], (DEPTH, 6 * D), 0.01),
        'norm_mix_g': 1.0 + nrm(k[4], (DEPTH, D), 0.05),
        'norm_ffn_g': 1.0 + nrm(k[5], (DEPTH, D), 0.05),
        'even_w_in': nrm(k[6], (N_EVEN, D, EVEN_IN), D ** -0.5),
        'even_w_out': nrm(k[7], (N_EVEN, EVEN_MIX, D), EVEN_MIX ** -0.5),
        'diff_lambda': nrm(k[8], (N_EVEN, 4, HEAD_DIM), 0.1),
        'diff_subln_g': 1.0 + nrm(k[9], (N_EVEN, DIFF_VDIM), 0.05),
        'nsa_cmp_pos': nrm(k[10], (N_EVEN, 2, CMP_BLOCK, HEAD_DIM), 0.1),
        'nsa_cmp_w1': nrm(k[11], (N_EVEN, 2, CMP_BLOCK * HEAD_DIM, CMP_HIDDEN), (CMP_BLOCK * HEAD_DIM) ** -0.5),
        'nsa_cmp_w2': nrm(k[12], (N_EVEN, 2, CMP_HIDDEN, HEAD_DIM), CMP_HIDDEN ** -0.5),
        'nsa_gate_b': nrm(k[13], (N_EVEN, NSA_GATE), 0.01),
        'odd_w_in': nrm(k[14], (N_ODD, D, ODD_IN), D ** -0.5),
        'odd_w_out': nrm(k[15], (N_ODD, FOX_W, D), FOX_W ** -0.5),
        'fox_f_b': jax.random.uniform(k[16], (N_ODD, FOX_HEADS), jnp.float32, 1.0, 4.0),
        'router_w': nrm(k[17], (DEPTH, D, E), D ** -0.5),
        'router_b': nrm(k[18], (DEPTH, E), 0.01),
        'moe_w_gate': nrm(k[19], (DEPTH, E, D, F), D ** -0.5),
        'moe_b_gate': nrm(k[20], (DEPTH, E, F), 0.01),
        'moe_w_up': nrm(k[21], (DEPTH, E, D, F), D ** -0.5),
        'moe_b_up': nrm(k[22], (DEPTH, E, F), 0.01),
        'moe_w_down': nrm(k[23], (DEPTH, E, F, D), F ** -0.5),
        'moe_b_down': nrm(k[24], (DEPTH, E, D), 0.01),
        'final_g': 1.0 + nrm(k[25], (D,), 0.05),
    }


def reference(x, c, ada_w, ada_b, norm_mix_g, norm_ffn_g, even_w_in, even_w_out,
              diff_lambda, diff_subln_g, nsa_cmp_pos, nsa_cmp_w1, nsa_cmp_w2, nsa_gate_b,
              odd_w_in, odd_w_out, fox_f_b, router_w, router_b, moe_w_gate, moe_b_gate,
              moe_w_up, moe_b_up, moe_w_down, moe_b_down, final_g):
    cond = jax.nn.silu(c)
    for l in range(DEPTH):
        mod = (cond @ ada_w[l] + ada_b[l])[:, None, :]
        sh1, sc1, g1, sh2, sc2, g2 = jnp.split(mod, 6, axis=-1)
        h = rmsnorm(x, norm_mix_g[l]) * (1.0 + sc1) + sh1
        if l % 2 == 0:
            e = l // 2
            lam_init = 0.8 - 0.6 * math.exp(-0.3 * l)
            m = even_mixer(h, even_w_in[e], even_w_out[e], diff_lambda[e], diff_subln_g[e],
                           nsa_cmp_pos[e], nsa_cmp_w1[e], nsa_cmp_w2[e], nsa_gate_b[e], lam_init)
        else:
            o = l // 2
            m = odd_mixer(h, odd_w_in[o], odd_w_out[o], fox_f_b[o])
        x = x + g1 * m
        h = rmsnorm(x, norm_ffn_g[l]) * (1.0 + sc2) + sh2
        x = x + g2 * moe(h, router_w[l], router_b[l], moe_w_gate[l], moe_b_gate[l],
                         moe_w_up[l], moe_b_up[l], moe_w_down[l], moe_b_down[l])
    return rmsnorm(x, final_g)
```

```python
import functools
import math

import numpy as np
import jax
import jax.numpy as jnp
from jax import lax
from jax.experimental import pallas as pl
from jax.experimental.pallas import tpu as pltpu

F32 = jnp.float32
BF16 = jnp.bfloat16
I32 = jnp.int32

HEAD_DIM = 64
EPS = 1e-6
NEG = -1e30
CMP_BLOCK = 32
CMP_STRIDE = 16
SLC_BLOCK = 64
SLC_TOPK = 16
WINDOW = 512
FORCE_BONUS = 1e4
N_EXPERTS = 32
TOP_K = 4
SWIGLU_LIMIT = 7.0
SWIGLU_ALPHA = 1.702
QK_SCALE = HEAD_DIM ** -0.5

LANES = 128
VMEM_LIMIT_BYTES = 56 * 1024 * 1024

ROW_TILE = 512
ATT_TILE = 256
NSA_TILE = 128
MOE_BLOCK = 512
COMBINE_TILE = 256
CUM_TILE = 512


def _cparams(sem):
    return pltpu.CompilerParams(dimension_semantics=sem, vmem_limit_bytes=VMEM_LIMIT_BYTES)


def _dot(a, b):
    return jnp.dot(a, b, preferred_element_type=F32)


def _dot_nt(a, b):
    return lax.dot_general(a, b, (((1,), (1,)), ((), ())), preferred_element_type=F32)


def _split2(a):
    hi = a.astype(BF16)
    lo = (a - hi.astype(F32)).astype(BF16)
    return hi, lo


def _split3(a):
    hi = a.astype(BF16)
    r = a - hi.astype(F32)
    mid = r.astype(BF16)
    lo = (r - mid.astype(F32)).astype(BF16)
    return hi, mid, lo


def _dot_hi(a, b):
    ah, al = _split2(a)
    bh, bl = _split2(b)
    return _dot(ah, bh) + (_dot(ah, bl) + _dot(al, bh))


def _dot_exact_rhs(a, b_bf16):
    a1, a2, a3 = _split3(a)
    return _dot(a1, b_bf16) + (_dot(a2, b_bf16) + _dot(a3, b_bf16))


def _rms(x):
    return x * lax.rsqrt(jnp.mean(x * x, axis=-1, keepdims=True) + EPS)


def _ada_kernel(c_ref, w_ref, b_ref, o_ref):
    c = c_ref[...]
    cond = c * jax.nn.sigmoid(c)
    o_ref[0] = _dot_hi(cond, w_ref[0]) + b_ref[0]


def _ada_mod(c, ada_w, ada_b):
    depth, d, n = ada_w.shape
    b = c.shape[0]
    bp = 8
    cp = jnp.zeros((bp, d), F32).at[:b].set(c)
    tn = 1536
    out = pl.pallas_call(
        _ada_kernel,
        out_shape=jax.ShapeDtypeStruct((depth, bp, n), F32),
        grid=(depth, n // tn),
        in_specs=[
            pl.BlockSpec((bp, d), lambda l, j: (0, 0)),
            pl.BlockSpec((1, d, tn), lambda l, j: (l, 0, j)),
            pl.BlockSpec((1, 1, tn), lambda l, j: (l, 0, j)),
        ],
        out_specs=pl.BlockSpec((1, bp, tn), lambda l, j: (l, 0, j)),
        compiler_params=_cparams(("arbitrary", "arbitrary")),
        name="ada_mod",
    )(cp, ada_w, ada_b.reshape(depth, 1, n))
    return out[:, :b]


def _norm_proj_kernel(x_ref, sc_ref, sh_ref, g_ref, w_ref, wp_ref, *o_refs, segs):
    x = x_ref[...]
    h = (_rms(x) * g_ref[...]) * (1.0 + sc_ref[0]) + sh_ref[0]
    hb = h.astype(BF16)
    for o_ref, (a, b) in zip(o_refs[:-1], segs):
        o_ref[...] = _dot(hb, w_ref[:, a:b]).astype(o_ref.dtype)
    o_refs[-1][...] = _dot_hi(h, wp_ref[...])


def _norm_proj(x2, sc, sh, g, w_main, w_prec, segs, seg_dtypes, rows_per_batch):
    t, d = x2.shape
    tm = min(ROW_TILE, rows_per_batch)
    per_b = rows_per_batch // tm
    n_main = w_main.shape[1]
    n_prec = w_prec.shape[1]
    out_shapes = [jax.ShapeDtypeStruct((t, b - a), dt) for (a, b), dt in zip(segs, seg_dtypes)]
    out_shapes.append(jax.ShapeDtypeStruct((t, n_prec), F32))
    out_specs = [pl.BlockSpec((tm, b - a), lambda i: (i, 0)) for (a, b) in segs]
    out_specs.append(pl.BlockSpec((tm, n_prec), lambda i: (i, 0)))
    return pl.pallas_call(
        functools.partial(_norm_proj_kernel, segs=tuple(segs)),
        out_shape=out_shapes,
        grid=(t // tm,),
        in_specs=[
            pl.BlockSpec((tm, d), lambda i: (i, 0)),
            pl.BlockSpec((1, 1, d), lambda i: (i // per_b, 0, 0)),
            pl.BlockSpec((1, 1, d), lambda i: (i // per_b, 0, 0)),
            pl.BlockSpec((1, d), lambda i: (0, 0)),
            pl.BlockSpec((d, n_main), lambda i: (0, 0)),
            pl.BlockSpec((d, n_prec), lambda i: (0, 0)),
        ],
        out_specs=out_specs,
        compiler_params=_cparams(("arbitrary",)),
        name="norm_proj",
    )(x2, sc, sh, g, w_main, w_prec)


def _diff_attn_kernel(slopes_ref, lamp_ref, g_ref, q_ref, k_ref, v_ref, o_ref,
                      m_ref, l_ref, acc_ref, *, tq, lam_init):
    h = pl.program_id(1)
    i = pl.program_id(2)
    slope = slopes_ref[h]
    lp = lamp_ref[...]
    lam = (jnp.exp(jnp.sum(lp[0:1] * lp[1:2], axis=1, keepdims=True))
           - jnp.exp(jnp.sum(lp[2:3] * lp[3:4], axis=1, keepdims=True)) + lam_init)

    q01 = q_ref[0] * jnp.asarray(QK_SCALE, BF16)
    lane = lax.broadcasted_iota(I32, (tq, LANES), 1)
    zero = jnp.zeros_like(q01)
    qs = jnp.concatenate([jnp.where(lane < HEAD_DIM, q01, zero),
                          jnp.where(lane >= HEAD_DIM, q01, zero)], axis=0)

    row = lax.broadcasted_iota(I32, (2 * tq, tq), 0)
    t_idx = jnp.where(row >= tq, row - tq, row)
    k_idx = lax.broadcasted_iota(I32, (2 * tq, tq), 1)
    rel = slope * (k_idx - t_idx).astype(F32)

    m_ref[...] = jnp.full(m_ref.shape, NEG, F32)
    l_ref[...] = jnp.zeros(l_ref.shape, F32)
    acc_ref[...] = jnp.zeros(acc_ref.shape, F32)

    def tile(j, diagonal):
        start = pl.multiple_of(j * tq, tq)
        k_t = k_ref[0, pl.ds(start, tq), :]
        v_t = v_ref[0, pl.ds(start, tq), :]
        s = _dot_nt(qs, k_t) + rel
        if diagonal:
            s = jnp.where(k_idx <= t_idx, s, NEG)
            off = 0.0
        else:
            off = -slope * ((i - j) * tq).astype(F32)
        m_old = m_ref[...]
        m_new = jnp.maximum(m_old, jnp.max(s, axis=1, keepdims=True) + off)
        p = jnp.exp(s - (m_new - off))
        alpha = jnp.exp(m_old - m_new)
        l_ref[...] = alpha * l_ref[...] + jnp.sum(p, axis=1, keepdims=True)
        acc_ref[...] = alpha * acc_ref[...] + _dot(p.astype(BF16), v_t)
        m_ref[...] = m_new

    def body(j, c):
        tile(j, False)
        return c

    lax.fori_loop(0, i, body, 0)
    tile(i, True)

    o = acc_ref[...] / l_ref[...]
    dlt = o[:tq] - lam * o[tq:]
    y = _rms(dlt) * g_ref[...] * (1.0 - lam_init)
    o_ref[0] = y.astype(o_ref.dtype)


def _diff_attention(q, k, v, slopes, lam_p, subln_g, lam_init):
    b, s, hw = q.shape
    nh = hw // LANES
    tq = min(ATT_TILE, s)
    return pl.pallas_call(
        functools.partial(_diff_attn_kernel, tq=tq, lam_init=lam_init),
        out_shape=jax.ShapeDtypeStruct((b, s, hw), BF16),
        grid=(b, nh, s // tq),
        in_specs=[
            pl.BlockSpec(memory_space=pltpu.SMEM),
            pl.BlockSpec((4, HEAD_DIM), lambda bi, h, i: (0, 0)),
            pl.BlockSpec((1, LANES), lambda bi, h, i: (0, 0)),
            pl.BlockSpec((1, tq, LANES), lambda bi, h, i: (bi, i, h)),
            pl.BlockSpec((1, s, LANES), lambda bi, h, i: (bi, 0, h)),
            pl.BlockSpec((1, s, LANES), lambda bi, h, i: (bi, 0, h)),
        ],
        out_specs=pl.BlockSpec((1, tq, LANES), lambda bi, h, i: (bi, i, h)),
        scratch_shapes=[pltpu.VMEM((2 * tq, 1), F32), pltpu.VMEM((2 * tq, 1), F32),
                        pltpu.VMEM((2 * tq, LANES), F32)],
        compiler_params=_cparams(("arbitrary", "arbitrary", "arbitrary")),
        name="diff_attn",
    )(slopes, lam_p, subln_g, q, k, v)


def _compress_kernel(r_ref, pos_ref, w1_ref, w2_ref, o_ref, *, nr):
    half = CMP_STRIDE * HEAD_DIM
    r = r_ref[0, 0, 0]
    pos = pos_ref[0]
    a = _dot((r + pos[:, :half]).astype(BF16), w1_ref[0, :half, :])
    bm = _dot((r + pos[:, half:]).astype(BF16), w1_ref[0, half:, :])
    pre = a + pltpu.roll(bm, nr - 1, 0)
    hid = pre * jax.nn.sigmoid(pre)
    o_ref[0, 0, 0] = _dot(hid.astype(BF16), w2_ref[0]).astype(o_ref.dtype)


def _compress(r, pos, w1, w2d):
    b, _, g, nr, width = r.shape
    return pl.pallas_call(
        functools.partial(_compress_kernel, nr=nr),
        out_shape=jax.ShapeDtypeStruct((b, 2, g, nr, LANES), BF16),
        grid=(b, 2, g),
        in_specs=[
            pl.BlockSpec((1, 1, 1, nr, width), lambda bi, kv, gi: (bi, kv, gi, 0, 0)),
            pl.BlockSpec((1, 1, 2 * width), lambda bi, kv, gi: (kv, 0, 0)),
            pl.BlockSpec((1, 2 * width, LANES), lambda bi, kv, gi: (kv, 0, 0)),
            pl.BlockSpec((1, LANES, LANES), lambda bi, kv, gi: (kv, 0, 0)),
        ],
        out_specs=pl.BlockSpec((1, 1, 1, nr, LANES), lambda bi, kv, gi: (bi, kv, gi, 0, 0)),
        compiler_params=_cparams(("arbitrary", "arbitrary", "arbitrary")),
        name="nsa_compress",
    )(r, pos, w1, w2d)


def _nsa_kernel(slopes_ref, q_ref, cmp_ref, ks_ref, vs_ref, kw_ref, vw_ref, gl_ref, gb_ref, ov_ref,
                o_ref, m_ref, l_ref, acc_ref, *, tq, nr, n_slc, top, hpg):
    g = pl.program_id(1)
    i = pl.program_id(2)
    nrow = hpg * tq
    blocks_per_tile = tq // SLC_BLOCK

    q = q_ref[0] * jnp.asarray(QK_SCALE, BF16)
    lane = lax.broadcasted_iota(I32, (tq, LANES), 1)
    zero = jnp.zeros((tq, LANES), BF16)
    parts = []
    for pr in range(hpg // 2):
        pq = q[:, pr * LANES:(pr + 1) * LANES]
        parts.append(jnp.where(lane < HEAD_DIM, pq, zero))
        parts.append(jnp.where(lane >= HEAD_DIM, pq, zero))
    lhs = jnp.concatenate(parts, axis=0)

    ri = lax.broadcasted_iota(I32, (nrow, 1), 0)
    t_col = jnp.bitwise_and(ri, tq - 1)
    slope_col = jnp.zeros((nrow, 1), F32)
    for hh in range(hpg):
        slope_col = jnp.where((ri >= hh * tq) & (ri < (hh + 1) * tq), slopes_ref[g * hpg + hh], slope_col)
    qpos_col = i * tq + t_col

    kc = cmp_ref[0, 0, 0]
    vc = cmp_ref[0, 1, 0]
    s = _dot_nt(lhs, kc)
    n_idx = lax.broadcasted_iota(I32, (nrow, nr), 1)
    dist = qpos_col - (n_idx * CMP_STRIDE + (CMP_BLOCK - 1))
    cmask = dist >= 0
    s = jnp.where(cmask, s - slope_col * dist.astype(F32), NEG)
    mx = jnp.max(s, axis=1, keepdims=True)
    p = jnp.where(cmask, jnp.exp(s - mx), 0.0)
    pc = p / jnp.maximum(jnp.sum(p, axis=1, keepdims=True), 1e-30)
    o_cmp = _dot(pc.astype(BF16), vc)
    imp_p = pc[0:tq]
    for hh in range(1, hpg):
        imp_p = imp_p + pc[hh * tq:(hh + 1) * tq]
    imp = _dot_exact_rhs(imp_p, ov_ref[...])

    jb = lax.broadcasted_iota(I32, (tq, n_slc), 1)
    tq_col = lax.broadcasted_iota(I32, (tq, 1), 0)
    cur = lax.shift_right_logical(i * tq + tq_col, int(math.log2(SLC_BLOCK)))
    avail = jb <= cur
    forced = (jb == 0) | (jb == cur) | (jb == cur - 1)
    score = jnp.where(avail, imp + jnp.where(forced, FORCE_BONUS, 0.0), NEG)
    rank = jnp.zeros((tq, n_slc), F32)
    for jp in range(n_slc):
        c = score[:, jp:jp + 1]
        beats = (c > score) | ((c == score) & (jb > jp))
        rank = rank + jnp.where(beats, 1.0, 0.0)
    sel = jnp.where((rank < float(top)) & avail, 1.0, 0.0).astype(BF16)

    k_idx = lax.broadcasted_iota(I32, (nrow, tq), 1)

    def flash(k_ref, v_ref, lo, hi, window):
        m_ref[...] = jnp.full(m_ref.shape, NEG, F32)
        l_ref[...] = jnp.zeros(l_ref.shape, F32)
        acc_ref[...] = jnp.zeros(acc_ref.shape, F32)

        def body(j, c):
            start = pl.multiple_of(j * tq, tq)
            k_t = k_ref[0, pl.ds(start, tq), :]
            v_t = v_ref[0, pl.ds(start, tq), :]
            d = qpos_col - (j * tq + k_idx)
            valid = d >= 0
            if window:
                valid = valid & (d < WINDOW)
            else:
                bi = lax.broadcasted_iota(I32, (n_slc, tq), 0)
                kk = lax.broadcasted_iota(I32, (n_slc, tq), 1)
                blk = j * blocks_per_tile + lax.shift_right_logical(kk, int(math.log2(SLC_BLOCK)))
                expand = jnp.where(bi == blk, 1.0, 0.0).astype(BF16)
                mt = _dot(sel, expand)
                valid = valid & (jnp.concatenate([mt] * hpg, axis=0) > 0.5)
            sc = _dot_nt(lhs, k_t) - slope_col * d.astype(F32)
            sc = jnp.where(valid, sc, NEG)
            m_old = m_ref[...]
            m_new = jnp.maximum(m_old, jnp.max(sc, axis=1, keepdims=True))
            pt = jnp.where(valid, jnp.exp(sc - m_new), 0.0)
            alpha = jnp.exp(m_old - m_new)
            l_ref[...] = alpha * l_ref[...] + jnp.sum(pt, axis=1, keepdims=True)
            acc_ref[...] = alpha * acc_ref[...] + _dot(pt.astype(BF16), v_t)
            m_ref[...] = m_new
            return c

        lax.fori_loop(lo, hi, body, 0)
        return acc_ref[...] / jnp.maximum(l_ref[...], 1e-30)

    o_slc = flash(ks_ref, vs_ref, 0, i + 1, False)
    o_win = flash(kw_ref, vw_ref, jnp.maximum(i - WINDOW // tq, 0), i + 1, True)

    gate = jax.nn.sigmoid(gl_ref[0] + gb_ref[...])
    ci = lax.broadcasted_iota(I32, (LANES, hpg * HEAD_DIM), 0)
    hd = lax.shift_right_logical(lax.broadcasted_iota(I32, (LANES, hpg * HEAD_DIM), 1),
                                 int(math.log2(HEAD_DIM)))
    g1, g2, g3 = _split3(gate)

    def assemble(o):
        pairs = []
        for pr in range(hpg // 2):
            lo_h = o[(2 * pr) * tq:(2 * pr + 1) * tq]
            hi_h = o[(2 * pr + 1) * tq:(2 * pr + 2) * tq]
            pairs.append(jnp.where(lane < HEAD_DIM, lo_h, hi_h))
        return jnp.concatenate(pairs, axis=1)

    out = jnp.zeros((tq, hpg * HEAD_DIM), F32)
    for br, o in enumerate((o_cmp, o_slc, o_win)):
        pick = jnp.where(ci == g * (3 * hpg) + hd * 3 + br, 1.0, 0.0).astype(BF16)
        gexp = _dot(g1, pick) + (_dot(g2, pick) + _dot(g3, pick))
        out = out + gexp * assemble(o)
    o_ref[0] = out.astype(o_ref.dtype)


def _nsa_attention(bq, cmp_kv, ks, vs, kw, vw, gl, gb, slopes, overlap, groups):
    b, s, qw = bq.shape
    hpg = qw // (groups * HEAD_DIM)
    tq = min(NSA_TILE, s)
    nr = cmp_kv.shape[3]
    n_slc = s // SLC_BLOCK
    top = min(SLC_TOPK, n_slc)
    gw = hpg * HEAD_DIM
    nrow = hpg * tq
    kv_spec = pl.BlockSpec((1, s, LANES), lambda bi, g, i: (bi, 0, g))
    return pl.pallas_call(
        functools.partial(_nsa_kernel, tq=tq, nr=nr, n_slc=n_slc, top=top, hpg=hpg),
        out_shape=jax.ShapeDtypeStruct((b, s, qw), BF16),
        grid=(b, groups, s // tq),
        in_specs=[
            pl.BlockSpec(memory_space=pltpu.SMEM),
            pl.BlockSpec((1, tq, gw), lambda bi, g, i: (bi, i, g)),
            pl.BlockSpec((1, 2, 1, nr, LANES), lambda bi, g, i: (bi, 0, g, 0, 0)),
            kv_spec, kv_spec, kv_spec, kv_spec,
            pl.BlockSpec((1, tq, LANES), lambda bi, g, i: (bi, i, 0)),
            pl.BlockSpec((1, LANES), lambda bi, g, i: (0, 0)),
            pl.BlockSpec((nr, n_slc), lambda bi, g, i: (0, 0)),
        ],
        out_specs=pl.BlockSpec((1, tq, gw), lambda bi, g, i: (bi, i, g)),
        scratch_shapes=[pltpu.VMEM((nrow, 1), F32), pltpu.VMEM((nrow, 1), F32),
                        pltpu.VMEM((nrow, LANES), F32)],
        compiler_params=_cparams(("arbitrary", "arbitrary", "arbitrary")),
        name="nsa_attn",
    )(slopes, bq, cmp_kv, ks, vs, kw, vw, gl, gb, overlap)


def _fox_cum_kernel(f_ref, fb_ref, o_ref, carry_ref, *, ts):
    @pl.when(pl.program_id(1) == 0)
    def _():
        carry_ref[...] = jnp.zeros(carry_ref.shape, F32)

    ls = jax.nn.log_sigmoid(f_ref[0] + fb_ref[...])
    r = lax.broadcasted_iota(I32, (ts, ts), 0)
    c = lax.broadcasted_iota(I32, (ts, ts), 1)
    tri = jnp.where(c <= r, 1.0, 0.0).astype(BF16)
    x1, x2, x3 = _split3(ls)
    cum = _dot(tri, x1) + (_dot(tri, x2) + _dot(tri, x3)) + carry_ref[...]
    o_ref[0] = cum
    carry_ref[...] = cum[ts - 1:ts, :]


def _fox_cumsum(f, fb):
    b, s, w = f.shape
    ts = min(CUM_TILE, s)
    return pl.pallas_call(
        functools.partial(_fox_cum_kernel, ts=ts),
        out_shape=jax.ShapeDtypeStruct((b, s, w), F32),
        grid=(b, s // ts),
        in_specs=[pl.BlockSpec((1, ts, w), lambda bi, i: (bi, i, 0)),
                  pl.BlockSpec((1, w), lambda bi, i: (0, 0))],
        out_specs=pl.BlockSpec((1, ts, w), lambda bi, i: (bi, i, 0)),
        scratch_shapes=[pltpu.VMEM((1, w), F32)],
        compiler_params=_cparams(("arbitrary", "arbitrary")),
        name="fox_cumsum",
    )(f, fb)


def _fox_attn_kernel(q_ref, k_ref, v_ref, cq_ref, ck_ref, o_ref, m_ref, l_ref, acc_ref, *, tq):
    i = pl.program_id(2)
    q01 = q_ref[0] * jnp.asarray(QK_SCALE, BF16)
    lane = lax.broadcasted_iota(I32, (tq, LANES), 1)
    zero = jnp.zeros_like(q01)
    qs = jnp.concatenate([jnp.where(lane < HEAD_DIM, q01, zero),
                          jnp.where(lane >= HEAD_DIM, q01, zero)], axis=0)
    cq = cq_ref[0]
    row = lax.broadcasted_iota(I32, (2, tq, tq), 1)
    col = lax.broadcasted_iota(I32, (2, tq, tq), 2)

    m_ref[...] = jnp.full(m_ref.shape, NEG, F32)
    l_ref[...] = jnp.zeros(l_ref.shape, F32)
    acc_ref[...] = jnp.zeros(acc_ref.shape, F32)

    def tile(j, diagonal):
        start = pl.multiple_of(j * tq, tq)
        k_t = k_ref[0, pl.ds(start, tq), :]
        v_t = v_ref[0, pl.ds(start, tq), :]
        ck = ck_ref[0, :, :, pl.ds(start, tq)]
        s = _dot_nt(qs, k_t).reshape(2, tq, tq) + (cq - ck)
        if diagonal:
            s = jnp.where(col <= row, s, NEG)
        s = s.reshape(2 * tq, tq)
        m_old = m_ref[...]
        m_new = jnp.maximum(m_old, jnp.max(s, axis=1, keepdims=True))
        p = jnp.exp(s - m_new)
        alpha = jnp.exp(m_old - m_new)
        l_ref[...] = alpha * l_ref[...] + jnp.sum(p, axis=1, keepdims=True)
        acc_ref[...] = alpha * acc_ref[...] + _dot(p.astype(BF16), v_t)
        m_ref[...] = m_new

    def body(j, c):
        tile(j, False)
        return c

    lax.fori_loop(0, i, body, 0)
    tile(i, True)
    o = acc_ref[...] / l_ref[...]
    o_ref[0] = jnp.where(lane < HEAD_DIM, o[:tq], o[tq:]).astype(o_ref.dtype)


def _fox_attention(q, k, v, cq, ck):
    b, s, w = q.shape
    npair = w // LANES
    tq = min(ATT_TILE, s)
    return pl.pallas_call(
        functools.partial(_fox_attn_kernel, tq=tq),
        out_shape=jax.ShapeDtypeStruct((b, s, w), BF16),
        grid=(b, npair, s // tq),
        in_specs=[
            pl.BlockSpec((1, tq, LANES), lambda bi, p, i: (bi, i, p)),
            pl.BlockSpec((1, s, LANES), lambda bi, p, i: (bi, 0, p)),
            pl.BlockSpec((1, s, LANES), lambda bi, p, i: (bi, 0, p)),
            pl.BlockSpec((1, 2, tq, 1), lambda bi, p, i: (bi, p, i, 0)),
            pl.BlockSpec((1, 2, 1, s), lambda bi, p, i: (bi, p, 0, 0)),
        ],
        out_specs=pl.BlockSpec((1, tq, LANES), lambda bi, p, i: (bi, i, p)),
        scratch_shapes=[pltpu.VMEM((2 * tq, 1), F32), pltpu.VMEM((2 * tq, 1), F32),
                        pltpu.VMEM((2 * tq, LANES), F32)],
        compiler_params=_cparams(("arbitrary", "arbitrary", "arbitrary")),
        name="fox_attn",
    )(q, k, v, cq, ck)


def _outproj_router_kernel(*refs, n_in):
    o_refs = refs[:n_in]
    w_refs = refs[n_in:2 * n_in]
    x_ref, g1_ref, gn_ref, sc_ref, sh_ref, rw_ref, rb_ref = refs[2 * n_in:2 * n_in + 7]
    xo_ref, h_ref, idx_ref, gate_ref = refs[2 * n_in + 7:]
    mix = _dot(o_refs[0][...], w_refs[0][...])
    for o_r, w_r in zip(o_refs[1:], w_refs[1:]):
        mix = mix + _dot(o_r[...], w_r[...])
    xn = x_ref[...] + g1_ref[0] * mix
    xo_ref[...] = xn
    h = (_rms(xn) * gn_ref[...]) * (1.0 + sc_ref[0]) + sh_ref[0]
    h_ref[...] = h
    logits = _dot_hi(h, rw_ref[...]) + rb_ref[...]
    lane = lax.broadcasted_iota(I32, logits.shape, 1)
    cur = logits
    vals, idxs = [], []
    for _ in range(TOP_K):
        mx = jnp.max(cur, axis=1, keepdims=True)
        ix = jnp.min(jnp.where(cur == mx, lane, LANES), axis=1, keepdims=True)
        vals.append(mx)
        idxs.append(ix)
        cur = jnp.where(lane == ix, 2.0 * NEG, cur)
    es = [jnp.exp(v - vals[0]) for v in vals]
    den = es[0]
    for e in es[1:]:
        den = den + e
    idx_out = jnp.zeros(logits.shape, I32)
    gate_out = jnp.zeros(logits.shape, F32)
    for k in range(TOP_K):
        idx_out = jnp.where(lane == k, idxs[k], idx_out)
        gate_out = jnp.where(lane == k, es[k] / den, gate_out)
    idx_ref[...] = idx_out
    gate_ref[...] = gate_out


def _outproj_router(o_list, w_list, x2, g1, gn, sc, sh, rw, rb, rows_per_batch):
    t, d = x2.shape
    tm = min(ROW_TILE, rows_per_batch)
    per_b = rows_per_batch // tm
    n_in = len(o_list)
    row = lambda i: (i, 0)
    full = lambda i: (0, 0)
    per_batch = lambda i: (i // per_b, 0, 0)
    in_specs = [pl.BlockSpec((tm, o.shape[1]), row) for o in o_list]
    in_specs += [pl.BlockSpec(w.shape, full) for w in w_list]
    in_specs += [
        pl.BlockSpec((tm, d), row),
        pl.BlockSpec((1, 1, d), per_batch),
        pl.BlockSpec((1, d), full),
        pl.BlockSpec((1, 1, d), per_batch),
        pl.BlockSpec((1, 1, d), per_batch),
        pl.BlockSpec((d, LANES), full),
        pl.BlockSpec((1, LANES), full),
    ]
    return pl.pallas_call(
        functools.partial(_outproj_router_kernel, n_in=n_in),
        out_shape=[jax.ShapeDtypeStruct((t, d), F32), jax.ShapeDtypeStruct((t, d), F32),
                   jax.ShapeDtypeStruct((t, LANES), I32), jax.ShapeDtypeStruct((t, LANES), F32)],
        grid=(t // tm,),
        in_specs=in_specs,
        out_specs=[pl.BlockSpec((tm, d), row), pl.BlockSpec((tm, d), row),
                   pl.BlockSpec((tm, LANES), row), pl.BlockSpec((tm, LANES), row)],
        compiler_params=_cparams(("arbitrary",)),
        name="outproj_router",
    )(*o_list, *w_list, x2, g1, gn, sc, sh, rw, rb)


def _moe_kernel(be_ref, bv_ref, tok_ref, x_hbm, wg_ref, bg_ref, wu_ref, bu_ref, wd_ref, bd_ref,
                y_ref, xbuf, wgb, wub, wdb, sem, *, bm):
    i = pl.program_id(0)
    valid = bv_ref[i] > 0
    changed = (i == 0) | (be_ref[i] != be_ref[jnp.maximum(i - 1, 0)])

    @pl.when(valid & changed)
    def _():
        wgb[...] = wg_ref[0].astype(BF16)
        wub[...] = wu_ref[0].astype(BF16)
        wdb[...] = wd_ref[0].astype(BF16)

    @pl.when(valid)
    def _():
        def issue(r, c):
            t = tok_ref[0, 0, r]
            pltpu.make_async_copy(x_hbm.at[pl.ds(t, 1), :], xbuf.at[pl.ds(r, 1), :], sem).start()
            return c

        lax.fori_loop(0, bm, issue, 0, unroll=8)
        pltpu.make_async_copy(x_hbm.at[pl.ds(0, bm), :], xbuf, sem).wait()
        xb = xbuf[...].astype(BF16)
        hg = jnp.minimum(_dot(xb, wgb[...]) + bg_ref[0], SWIGLU_LIMIT)
        hu = jnp.clip(_dot(xb, wub[...]) + bu_ref[0], -SWIGLU_LIMIT, SWIGLU_LIMIT)
        a = hg * jax.nn.sigmoid(SWIGLU_ALPHA * hg) * (hu + 1.0)
        y_ref[...] = _dot(a.astype(BF16), wdb[...]) + bd_ref[0]

    @pl.when(jnp.logical_not(valid))
    def _():
        y_ref[...] = jnp.zeros(y_ref.shape, F32)


def _moe_experts(block_expert, block_valid, tok_pad, h2, wg, bg, wu, bu, wd, bd):
    e, d, f = wg.shape
    nb = block_expert.shape[0]
    bm = MOE_BLOCK
    wmap = lambda i, be, bv: (be[i], 0, 0)
    grid_spec = pltpu.PrefetchScalarGridSpec(
        num_scalar_prefetch=2,
        grid=(nb,),
        in_specs=[
            pl.BlockSpec((1, 1, bm), lambda i, be, bv: (i, 0, 0), memory_space=pltpu.SMEM),
            pl.BlockSpec(memory_space=pl.ANY),
            pl.BlockSpec((1, d, f), wmap),
            pl.BlockSpec((1, 1, f), wmap),
            pl.BlockSpec((1, d, f), wmap),
            pl.BlockSpec((1, 1, f), wmap),
            pl.BlockSpec((1, f, d), wmap),
            pl.BlockSpec((1, 1, d), wmap),
        ],
        out_specs=pl.BlockSpec((bm, d), lambda i, be, bv: (i, 0)),
        scratch_shapes=[pltpu.VMEM((bm, d), F32), pltpu.VMEM((d, f), BF16), pltpu.VMEM((d, f), BF16),
                        pltpu.VMEM((f, d), BF16), pltpu.SemaphoreType.DMA],
    )
    return pl.pallas_call(
        functools.partial(_moe_kernel, bm=bm),
        out_shape=jax.ShapeDtypeStruct((nb * bm, d), F32),
        grid_spec=grid_spec,
        compiler_params=_cparams(("arbitrary",)),
        name="moe_experts",
    )(block_expert, block_valid, tok_pad.reshape(nb, 1, bm), h2, wg, bg.reshape(e, 1, f),
      wu, bu.reshape(e, 1, f), wd, bd.reshape(e, 1, d))


def _combine_kernel(dest_ref, y_hbm, gate_ref, x_ref, g2_ref, fg_ref, o_ref, ybuf, sem, *, tm, final):
    n = TOP_K * tm

    def issue(r, c):
        src = dest_ref[0, 0, r]
        pltpu.make_async_copy(y_hbm.at[pl.ds(src, 1), :], ybuf.at[pl.ds(r, 1), :], sem).start()
        return c

    lax.fori_loop(0, n, issue, 0, unroll=8)
    pltpu.make_async_copy(y_hbm.at[pl.ds(0, n), :], ybuf, sem).wait()
    gate = gate_ref[...]
    acc = gate[:, 0:1] * ybuf[0:tm, :]
    for k in range(1, TOP_K):
        acc = acc + gate[:, k:k + 1] * ybuf[k * tm:(k + 1) * tm, :]
    xn = x_ref[...] + g2_ref[0] * acc
    if final:
        xn = _rms(xn) * fg_ref[...]
    o_ref[...] = xn


def _moe_combine(dest_tiles, y, gate, x2, g2, final_g, rows_per_batch, final):
    t, d = x2.shape
    tm = min(COMBINE_TILE, rows_per_batch)
    per_b = rows_per_batch // tm
    return pl.pallas_call(
        functools.partial(_combine_kernel, tm=tm, final=final),
        out_shape=jax.ShapeDtypeStruct((t, d), F32),
        grid=(t // tm,),
        in_specs=[
            pl.BlockSpec((1, 1, TOP_K * tm), lambda i: (i, 0, 0), memory_space=pltpu.SMEM),
            pl.BlockSpec(memory_space=pl.ANY),
            pl.BlockSpec((tm, LANES), lambda i: (i, 0)),
            pl.BlockSpec((tm, d), lambda i: (i, 0)),
            pl.BlockSpec((1, 1, d), lambda i: (i // per_b, 0, 0)),
            pl.BlockSpec((1, d), lambda i: (0, 0)),
        ],
        out_specs=pl.BlockSpec((tm, d), lambda i: (i, 0)),
        scratch_shapes=[pltpu.VMEM((TOP_K * tm, d), F32), pltpu.SemaphoreType.DMA],
        compiler_params=_cparams(("arbitrary",)),
        name="moe_combine",
    )(dest_tiles, y, gate, x2, g2, final_g)


def _dispatch_tables(idx, t):
    bm = MOE_BLOCK
    m = t * TOP_K
    nb = m // bm + N_EXPERTS
    onehot = (idx[:, :, None] == jnp.arange(N_EXPERTS, dtype=I32)[None, None, :]).astype(I32).sum(axis=1)
    incl = jnp.cumsum(onehot, axis=0)
    excl = incl - onehot
    counts = incl[-1]
    nblk = (counts + bm - 1) // bm
    pend = jnp.cumsum(nblk)
    pstart = (pend - nblk) * bm
    dest = pstart[idx] + jnp.take_along_axis(excl, idx, axis=1)
    tok = jnp.broadcast_to(jnp.arange(t, dtype=I32)[:, None], (t, TOP_K))
    tok_pad = jnp.zeros((nb * bm,), I32).at[dest.reshape(-1)].set(tok.reshape(-1))
    blocks = jnp.arange(nb, dtype=I32)
    block_expert = jnp.minimum(jnp.searchsorted(pend, blocks, side="right"), N_EXPERTS - 1).astype(I32)
    block_valid = (blocks < pend[-1]).astype(I32)
    return dest.astype(I32), tok_pad, block_expert, block_valid


def _moe_layer(x2, h2, idx, gate, g2, final_g, params, rows_per_batch, final):
    t, d = x2.shape
    wg, bg, wu, bu, wd, bd = params
    dest, tok_pad, block_expert, block_valid = _dispatch_tables(idx[:, :TOP_K], t)
    y = _moe_experts(block_expert, block_valid, tok_pad, h2, wg, bg, wu, bu, wd, bd)
    tm = min(COMBINE_TILE, rows_per_batch)
    dest_tiles = dest.reshape(t // tm, tm, TOP_K).transpose(0, 2, 1).reshape(t // tm, 1, TOP_K * tm)
    return _moe_combine(dest_tiles, y, gate, x2, g2, final_g, rows_per_batch, final)


def _alibi_slopes(n):
    return jnp.asarray(2.0 ** (-8.0 * np.arange(1, n + 1) / n), dtype=F32)


def _pad_cols(w, n):
    return jnp.zeros((w.shape[0], n), w.dtype).at[:, :w.shape[1]].set(w)


def _overlap_matrix(nr, n_slc):
    ci = np.arange(nr)[:, None] * CMP_STRIDE
    sj = np.arange(n_slc)[None, :] * SLC_BLOCK
    ov = ((ci < sj + SLC_BLOCK) & (ci + CMP_BLOCK > sj)).astype(np.float32)
    ov[nr - 1] = 0.0
    return jnp.asarray(ov, dtype=BF16)


def _even_mixer(x2, b, s, sc, sh, g, w_in, lam_p, subln_g, cmp_pos, cmp_w1, cmp_w2, gate_b, lam_init):
    d = x2.shape[1]
    dh = HEAD_DIM
    diff_heads = d // 256
    nsa_heads = d // 128
    groups = max(1, nsa_heads // 4)
    dq = diff_heads * 2 * dh
    dv = diff_heads * 2 * dh
    nq = nsa_heads * dh
    nkv = groups * dh
    offs = np.cumsum([0, dq, dq, dv, nq, nkv, nkv, nkv, nkv, nkv, nkv, 3 * nsa_heads]).tolist()
    col = lambda a: w_in[:, offs[a]:offs[a + 1]]

    def dup(w):
        return jnp.repeat(w.reshape(d, groups, 1, dh), 2, axis=2).reshape(d, groups * 2 * dh)

    w_main = jnp.concatenate([col(0), col(1), col(2), col(3), dup(col(6)), dup(col(7)), dup(col(8)),
                              dup(col(9)), col(4), col(5)], axis=1).astype(BF16)
    widths = [dq, dq, dv, nq, 2 * nkv, 2 * nkv, 2 * nkv, 2 * nkv, 2 * nkv]
    starts = np.cumsum([0] + widths).tolist()
    segs = [(starts[k], starts[k + 1]) for k in range(len(widths))]
    dts = [BF16] * 8 + [F32]
    w_prec = _pad_cols(col(10), LANES)
    aq, ak, av, bq, ks, vs, kw, vw, kvc, gl = _norm_proj(x2, sc, sh, g, w_main, w_prec, segs, dts, s)

    o_a = _diff_attention(aq.reshape(b, s, dq), ak.reshape(b, s, dq), av.reshape(b, s, dv),
                          _alibi_slopes(diff_heads), lam_p, subln_g.reshape(1, 2 * dh), lam_init)

    nr = s // CMP_STRIDE
    r = kvc.reshape(b, nr, CMP_STRIDE, 2, groups, dh).transpose(0, 3, 4, 1, 2, 5)
    r = r.reshape(b, 2, groups, nr, CMP_STRIDE * dh)
    w2d = jnp.concatenate([cmp_w2, cmp_w2], axis=-1).astype(BF16)
    cmp_kv = _compress(r, cmp_pos.reshape(2, 1, CMP_BLOCK * dh), cmp_w1.astype(BF16), w2d)

    n_slc = s // SLC_BLOCK
    o_b = _nsa_attention(bq.reshape(b, s, nq), cmp_kv, ks.reshape(b, s, 2 * nkv), vs.reshape(b, s, 2 * nkv),
                         kw.reshape(b, s, 2 * nkv), vw.reshape(b, s, 2 * nkv), gl.reshape(b, s, LANES),
                         _pad_cols(gate_b.reshape(1, -1), LANES), _alibi_slopes(nsa_heads),
                         _overlap_matrix(nr, n_slc), groups)
    return [o_a.reshape(b * s, dv), o_b.reshape(b * s, nq)]


def _odd_mixer(x2, b, s, sc, sh, g, w_in, f_b):
    d = x2.shape[1]
    heads = d // HEAD_DIM
    w_main = w_in[:, :3 * d].astype(BF16)
    w_prec = _pad_cols(w_in[:, 3 * d:], LANES)
    segs = [(0, d), (d, 2 * d), (2 * d, 3 * d)]
    q, k, v, f = _norm_proj(x2, sc, sh, g, w_main, w_prec, segs, [BF16] * 3, s)
    cum = _fox_cumsum(f.reshape(b, s, LANES), _pad_cols(f_b.reshape(1, -1), LANES))[:, :, :heads]
    cum_t = jnp.swapaxes(cum, 1, 2)
    o = _fox_attention(q.reshape(b, s, d), k.reshape(b, s, d), v.reshape(b, s, d),
                       cum_t[:, :, :, None], cum_t[:, :, None, :])
    return [o.reshape(b * s, d)]


def kernel(x, c, ada_w, ada_b, norm_mix_g, norm_ffn_g, even_w_in, even_w_out, diff_lambda, diff_subln_g,
           nsa_cmp_pos, nsa_cmp_w1, nsa_cmp_w2, nsa_gate_b, odd_w_in, odd_w_out, fox_f_b, router_w, router_b,
           moe_w_gate, moe_b_gate, moe_w_up, moe_b_up, moe_w_down, moe_b_down, final_g):
    b, s, d = x.shape
    depth = ada_w.shape[0]
    mod = _ada_mod(c, ada_w, ada_b)
    x2 = x.reshape(b * s, d)
    fg = final_g.reshape(1, d)
    for l in range(depth):
        sh1, sc1, g1, sh2, sc2, g2 = [mod[l, :, k * d:(k + 1) * d].reshape(b, 1, d) for k in range(6)]
        gm = norm_mix_g[l].reshape(1, d)
        if l % 2 == 0:
            e = l // 2
            lam_init = 0.8 - 0.6 * math.exp(-0.3 * l)
            outs = _even_mixer(x2, b, s, sc1, sh1, gm, even_w_in[e], diff_lambda[e], diff_subln_g[e],
                               nsa_cmp_pos[e], nsa_cmp_w1[e], nsa_cmp_w2[e], nsa_gate_b[e], lam_init)
            w_out = even_w_out[e].astype(BF16)
        else:
            o = l // 2
            outs = _odd_mixer(x2, b, s, sc1, sh1, gm, odd_w_in[o], fox_f_b[o])
            w_out = odd_w_out[o].astype(BF16)
        w_list, r0 = [], 0
        for o_arr in outs:
            w_list.append(w_out[r0:r0 + o_arr.shape[1]])
            r0 += o_arr.shape[1]
        rw = _pad_cols(router_w[l], LANES)
        rb = jnp.full((1, LANES), NEG, F32).at[0, :N_EXPERTS].set(router_b[l])
        x2, h2, idx, gate = _outproj_router(outs, w_list, x2, g1, norm_ffn_g[l].reshape(1, d), sc2, sh2,
                                            rw, rb, s)
        x2 = _moe_layer(x2, h2, idx, gate, g2, fg,
                        (moe_w_gate[l], moe_b_gate[l], moe_w_up[l], moe_b_up[l], moe_w_down[l], moe_b_down[l]),
                        s, final=(l == depth - 1))
    return x2.reshape(b, s, d)
```
